```python
import jax, jax.numpy as jnp
from jax import lax
import numpy as np

D_MODEL = 1024
BATCH = 2
SEQ = 16384
DEPTH = 1

GLA_HEADS = 4
GLA_DK = D_MODEL // 2 // GLA_HEADS
GLA_DV = D_MODEL // GLA_HEADS
GLA_RANK = 16
GLA_GATE_NORM = 16.0
GDN_HEADS = 8
GDN_DK = D_MODEL // GDN_HEADS
GDN_DV = D_MODEL // GDN_HEADS
CONV_K = 4
CHUNK = 64
D_FF = 4 * D_MODEL
EPS = 1e-6

GLA_QK = GLA_HEADS * GLA_DK
GLA_V = GLA_HEADS * GLA_DV
GDN_QK = GDN_HEADS * GDN_DK
GDN_V = GDN_HEADS * GDN_DV
GDN_CONV_DIM = 2 * GDN_QK + GDN_V
IN_WIDTHS = (GLA_QK, GLA_QK, GLA_V, GLA_V, GLA_RANK,
             GDN_CONV_DIM, GDN_V, GDN_HEADS, GDN_HEADS,
             D_MODEL, D_MODEL)
IN_DIM = 2 * GLA_QK + 2 * GLA_V + GLA_RANK + GDN_CONV_DIM + GDN_V + 2 * GDN_HEADS + 2 * D_MODEL

kernel_name = 'hybrid_gla_gdn_gated_merge_block'


def rms_norm(x, w):
    xf = x.astype(jnp.float32)
    y = xf * lax.rsqrt(jnp.mean(xf * xf, axis=-1, keepdims=True) + EPS)
    return (y * w.astype(jnp.float32)).astype(x.dtype)


def l2_norm(x):
    xf = x.astype(jnp.float32)
    return (xf * lax.rsqrt(jnp.sum(xf * xf, axis=-1, keepdims=True) + EPS)).astype(x.dtype)


def causal_short_conv(x, w):
    s = x.shape[1]
    k = w.shape[0]
    xp = jnp.pad(x, ((0, 0), (k - 1, 0), (0, 0)))
    y = xp[:, 0:s] * w[0]
    for j in range(1, k):
        y = y + xp[:, j:j + s] * w[j]
    return y


def to_chunks(t):
    b, s, h = t.shape[:3]
    t = t.reshape((b, s // CHUNK, CHUNK, h) + t.shape[3:])
    return jnp.moveaxis(t, 3, 1)


def from_chunks(t):
    b, h, n, c, d = t.shape
    return jnp.transpose(t, (0, 2, 3, 1, 4)).reshape(b, n * c, h, d)


def gla_chunked(q, k, v, gk):
    out_dtype = v.dtype
    dk = q.shape[-1]
    q = to_chunks(q.astype(jnp.float32)) * (dk ** -0.5)
    k = to_chunks(k.astype(jnp.float32))
    v = to_chunks(v.astype(jnp.float32))
    b = jnp.cumsum(to_chunks(gk.astype(jnp.float32)), axis=3)
    b_last = b[:, :, :, -1:, :]
    q_dec = q * jnp.exp(b)
    k_inv = k * jnp.exp(-b)
    causal = jnp.tril(jnp.ones((CHUNK, CHUNK), dtype=bool))
    att = jnp.einsum('bhnid,bhnjd->bhnij', q_dec, k_inv)
    att = jnp.where(causal, att, 0.0)
    o_intra = jnp.einsum('bhnij,bhnjv->bhniv', att, v)
    k_state = k * jnp.exp(b_last - b)
    dec = jnp.exp(b_last[:, :, :, 0, :])

    def step(state, inp):
        qd, ks, vv, dc = inp
        o = jnp.einsum('bhcd,bhdv->bhcv', qd, state)
        state = dc[..., None] * state + jnp.einsum('bhcd,bhcv->bhdv', ks, vv)
        return state, o

    bsz, h = q.shape[0], q.shape[1]
    s0 = jnp.zeros((bsz, h, dk, v.shape[-1]), jnp.float32)
    xs = (jnp.moveaxis(q_dec, 2, 0), jnp.moveaxis(k_state, 2, 0),
          jnp.moveaxis(v, 2, 0), jnp.moveaxis(dec, 2, 0))
    _, o_inter = lax.scan(step, s0, xs)
    o = o_intra + jnp.moveaxis(o_inter, 0, 2)
    return from_chunks(o).astype(out_dtype)


def gated_delta_chunked(q, k, v, beta, g):
    out_dtype = v.dtype
    dk = q.shape[-1]
    q = to_chunks(q.astype(jnp.float32)) * (dk ** -0.5)
    k = to_chunks(k.astype(jnp.float32))
    v = to_chunks(v.astype(jnp.float32))
    beta = to_chunks(beta.astype(jnp.float32))
    gc = jnp.cumsum(to_chunks(g.astype(jnp.float32)), axis=-1)
    causal = jnp.tril(jnp.ones((CHUNK, CHUNK), dtype=bool))
    strict = jnp.tril(jnp.ones((CHUNK, CHUNK), dtype=bool), k=-1)
    diff = gc[..., :, None] - gc[..., None, :]
    decay_mat = jnp.exp(jnp.where(causal, diff, -jnp.inf))
    kb = k * beta[..., None]
    a_mat = jnp.where(strict, jnp.einsum('bhnid,bhnjd->bhnij', kb, k) * decay_mat, 0.0)
    eye = jnp.eye(CHUNK, dtype=jnp.float32)
    rhs = jnp.concatenate([v * beta[..., None], kb * jnp.exp(gc)[..., None]], axis=-1)
    sol = lax.linalg.triangular_solve(a_mat + eye, rhs, left_side=True, lower=True)
    dv = v.shape[-1]
    u = sol[..., :dv]
    w = sol[..., dv:]
    att = jnp.einsum('bhnid,bhnjd->bhnij', q, k) * decay_mat
    q_dec = q * jnp.exp(gc)[..., None]
    g_last = gc[..., -1]
    k_state = k * jnp.exp(g_last[..., None] - gc)[..., None]
    dec = jnp.exp(g_last)

    def step(state, inp):
        qd, at, uu, ww, ks, dc = inp
        v_new = uu - jnp.einsum('bhcd,bhdv->bhcv', ww, state)
        o = jnp.einsum('bhcd,bhdv->bhcv', qd, state) + jnp.einsum('bhij,bhjv->bhiv', at, v_new)
        state = dc[..., None, None] * state + jnp.einsum('bhcd,bhcv->bhdv', ks, v_new)
        return state, o

    bsz, h = q.shape[0], q.shape[1]
    s0 = jnp.zeros((bsz, h, dk, dv), jnp.float32)
    xs = tuple(jnp.moveaxis(t, 2, 0) for t in (q_dec, att, u, w, k_state, dec))
    _, o = lax.scan(step, s0, xs)
    return from_chunks(jnp.moveaxis(o, 0, 2)).astype(out_dtype)


def hybrid_layer(x, c, ada_w, ada_b, pre_mix_w, post_mix_w, pre_mlp_w, post_mlp_w,
                 w_in, gla_w_lr, gla_b_lr, gla_onorm_w, gdn_conv_w, gdn_a_log, gdn_dt_bias,
                 gdn_onorm_w, w_branch_gla, w_branch_gdn, w_out, mlp_w1, mlp_w2):
    bsz, s, d = x.shape
    mod = jax.nn.silu(c) @ ada_w + ada_b
    shift1, scale1, gate1, shift2, scale2, gate2 = [m[:, None, :] for m in jnp.split(mod, 6, axis=-1)]

    h = rms_norm(x, pre_mix_w) * (1.0 + scale1) + shift1
    proj = h @ w_in
    idx = [int(i) for i in np.cumsum(np.array(IN_WIDTHS))[:-1]]
    (a_q, a_k, a_v, a_g, a_lr, d_qkv, d_g, d_beta, d_a, m_a, m_d) = jnp.split(proj, idx, axis=-1)

    a_gk = jax.nn.log_sigmoid(a_lr @ gla_w_lr + gla_b_lr) / GLA_GATE_NORM
    o_a = gla_chunked(a_q.reshape(bsz, s, GLA_HEADS, GLA_DK),
                      a_k.reshape(bsz, s, GLA_HEADS, GLA_DK),
                      a_v.reshape(bsz, s, GLA_HEADS, GLA_DV),
                      a_gk.reshape(bsz, s, GLA_HEADS, GLA_DK))
    o_a = rms_norm(o_a, gla_onorm_w) * jax.nn.silu(a_g.reshape(bsz, s, GLA_HEADS, GLA_DV))
    y_a = o_a.reshape(bsz, s, GLA_V) @ w_branch_gla

    d_qkv = jax.nn.silu(causal_short_conv(d_qkv, gdn_conv_w))
    d_q, d_k, d_v = jnp.split(d_qkv, [GDN_QK, 2 * GDN_QK], axis=-1)
    d_q = l2_norm(d_q.reshape(bsz, s, GDN_HEADS, GDN_DK))
    d_k = l2_norm(d_k.reshape(bsz, s, GDN_HEADS, GDN_DK))
    beta = jax.nn.sigmoid(d_beta)
    g = -jnp.exp(gdn_a_log) * jax.nn.softplus(d_a + gdn_dt_bias)
    o_d = gated_delta_chunked(d_q, d_k, d_v.reshape(bsz, s, GDN_HEADS, GDN_DV), beta, g)
    o_d = rms_norm(o_d, gdn_onorm_w) * jax.nn.silu(d_g.reshape(bsz, s, GDN_HEADS, GDN_DV))
    y_d = o_d.reshape(bsz, s, GDN_V) @ w_branch_gdn

    merged = jax.nn.sigmoid(m_a) * y_a + jax.nn.sigmoid(m_d) * y_d
    x = x + gate1 * rms_norm(merged @ w_out, post_mix_w)

    h2 = rms_norm(x, pre_mlp_w) * (1.0 + scale2) + shift2
    y = jnp.square(jax.nn.relu(h2 @ mlp_w1)) @ mlp_w2
    x = x + gate2 * rms_norm(y, post_mlp_w)
    return x


def setup_inputs(seed: int = 0) -> dict:
    key = jax.random.key(seed)
    ks = jax.random.split(key, 24)
    L, D = DEPTH, D_MODEL
    f32 = jnp.float32

    def nrm(k, shape, scale):
        return jax.random.normal(k, shape, f32) * scale

    def gain(k, n):
        return 1.0 + 0.02 * jax.random.normal(k, (L, n), f32)

    dt = jnp.exp(jax.random.uniform(ks[13], (L, GDN_HEADS), f32, np.log(1e-3), np.log(1e-1)))
    return {
        'x': jax.random.normal(ks[0], (BATCH, SEQ, D), f32),
        'c': jax.random.normal(ks[1], (BATCH, D), f32),
        'ada_w': nrm(ks[2], (L, D, 6 * D), D ** -0.5),
        'ada_b': nrm(ks[3], (L, 6 * D), 0.02),
        'pre_mix_w': gain(ks[4], D),
        'post_mix_w': gain(ks[5], D),
        'pre_mlp_w': gain(ks[6], D),
        'post_mlp_w': gain(ks[7], D),
        'w_in': nrm(ks[8], (L, D, IN_DIM), D ** -0.5),
        'gla_w_lr': nrm(ks[9], (L, GLA_RANK, GLA_QK), GLA_RANK ** -0.5),
        'gla_b_lr': nrm(ks[10], (L, GLA_QK), 0.1),
        'gla_onorm_w': gain(ks[11], GLA_DV),
        'gdn_conv_w': nrm(ks[12], (L, CONV_K, GDN_CONV_DIM), CONV_K ** -0.5),
        'gdn_a_log': jnp.log(jax.random.uniform(ks[14], (L, GDN_HEADS), f32, 1.0, 16.0)),
        'gdn_dt_bias': dt + jnp.log(-jnp.expm1(-dt)),
        'gdn_onorm_w': gain(ks[15], GDN_DV),
        'w_branch_gla': nrm(ks[16], (L, GLA_V, D), GLA_V ** -0.5),
        'w_branch_gdn': nrm(ks[17], (L, GDN_V, D), GDN_V ** -0.5),
        'w_out': nrm(ks[18], (L, D, D), D ** -0.5),
        'mlp_w1': nrm(ks[19], (L, D, D_FF), D ** -0.5),
        'mlp_w2': nrm(ks[20], (L, D_FF, D), D_FF ** -0.5),
    }


def reference(x, c, ada_w, ada_b, pre_mix_w, post_mix_w, pre_mlp_w, post_mlp_w,
              w_in, gla_w_lr, gla_b_lr, gla_onorm_w, gdn_conv_w, gdn_a_log, gdn_dt_bias,
              gdn_onorm_w, w_branch_gla, w_branch_gdn, w_out, mlp_w1, mlp_w2):
    for l in range(DEPTH):
        x = hybrid_layer(x, c, ada_w[l], ada_b[l], pre_mix_w[l], post_mix_w[l], pre_mlp_w[l],
                         post_mlp_w[l], w_in[l], gla_w_lr[l], gla_b_lr[l], gla_onorm_w[l],
                         gdn_conv_w[l], gdn_a_log[l], gdn_dt_bias[l], gdn_onorm_w[l],
                         w_branch_gla[l], w_branch_gdn[l], w_out[l], mlp_w1[l], mlp_w2[l])
    return x
```

```python
import functools

import jax
import jax.numpy as jnp
from jax import lax
from jax.experimental import pallas as pl
from jax.experimental.pallas import tpu as pltpu

F32 = jnp.float32
BF16 = jnp.bfloat16

D_MODEL = 1024
GLA_HEADS, GLA_DK, GLA_DV, GLA_RANK = 4, 128, 256, 16
GLA_GATE_NORM = 16.0
GDN_HEADS, GDN_DK, GDN_DV = 8, 128, 128
CONV_K = 4
CHUNK = 64
D_FF = 4 * D_MODEL
EPS = 1e-6
GLA_QK = GLA_HEADS * GLA_DK
GLA_V = GLA_HEADS * GLA_DV
GDN_QK = GDN_HEADS * GDN_DK
GDN_V = GDN_HEADS * GDN_DV

LANES = 128
SUBLANES = 8
VMEM_LIMIT = 56 * 1024 * 1024

TOKEN_TILE = 512
GROUP = 256
NEG_BIG = -1e30

A_Q, A_K, A_V, A_G, A_MA, A_MD, A_LR, A_END = 0, 512, 1024, 2048, 3072, 4096, 5120, 5248
B_Q, B_K, B_V, B_G, B_SM, B_END = 0, 1024, 2048, 3072, 4096, 4224
SM_BETA, SM_GC, SM_EGC, SM_EKL, SM_DEC = 0, 8, 16, 24, 32


def _dot(a, b):
    return jnp.dot(a, b, preferred_element_type=F32)


def _dot_nt(a, b):
    return lax.dot_general(a, b, (((1,), (1,)), ((), ())), preferred_element_type=F32)


def _dot_tn(a, b):
    return lax.dot_general(a, b, (((0,), (0,)), ((), ())), preferred_element_type=F32)


def _sigmoid(x):
    return 1.0 / (1.0 + jnp.exp(-x))


def _silu(x):
    return x * _sigmoid(x)


def _softplus(x):
    return jnp.maximum(x, 0.0) + jnp.log(1.0 + jnp.exp(-jnp.abs(x)))


def _rms(x, w):
    return x * lax.rsqrt(jnp.mean(x * x, axis=-1, keepdims=True) + EPS) * w


def _split_dot(tri, x):
    hi = x.astype(BF16)
    lo = (x - hi.astype(F32)).astype(BF16)
    return _dot(tri, hi) + _dot(tri, lo)


def _ada_kernel(c_ref, w_ref, b_ref, o_ref):
    c = c_ref[...]
    o_ref[...] = _dot(_silu(c).astype(BF16), w_ref[...].astype(BF16)) + b_ref[...]


def _ada(c_pad, ada_w, ada_b):
    n = ada_w.shape[1]
    tn = 1024
    return pl.pallas_call(
        _ada_kernel,
        grid=(n // tn,),
        in_specs=[pl.BlockSpec((SUBLANES, D_MODEL), lambda j: (0, 0)),
                  pl.BlockSpec((D_MODEL, tn), lambda j: (0, j)),
                  pl.BlockSpec((1, tn), lambda j: (0, j))],
        out_specs=pl.BlockSpec((SUBLANES, tn), lambda j: (0, j)),
        out_shape=jax.ShapeDtypeStruct((SUBLANES, n), F32),
        name="ada",
    )(c_pad, ada_w, ada_b)


def _modulated_norm(x, mod, nw, first):
    shift = mod[first:first + 1]
    scale = mod[first + 1:first + 2]
    return _rms(x, nw) * (1.0 + scale) + shift


def _gla_in_kernel(x_ref, mod_ref, nw_ref, w_ref, wlr_ref, blr_ref, tri_ref,
                   qd_ref, ki_ref, ks_ref, v_ref, g_ref, ma_ref, md_ref, dec_ref):
    tm = x_ref.shape[0]
    h = _modulated_norm(x_ref[...], mod_ref[0], nw_ref[...], 0).astype(BF16)

    def proj(lo, hi):
        return _dot(h, w_ref[:, lo:hi])

    v_ref[...] = proj(A_V, A_G).astype(BF16)
    g_ref[...] = _silu(proj(A_G, A_MA)).astype(BF16)
    ma_ref[...] = _sigmoid(proj(A_MA, A_MD)).astype(BF16)
    md_ref[...] = _sigmoid(proj(A_MD, A_LR)).astype(BF16)

    z = _dot(proj(A_LR, A_END).astype(BF16), wlr_ref[...]) + blr_ref[...]
    gk = (jnp.minimum(z, 0.0) - jnp.log(1.0 + jnp.exp(-jnp.abs(z)))) * (1.0 / GLA_GATE_NORM)
    b = _split_dot(tri_ref[...], gk)
    q = proj(A_Q, A_K) * (GLA_DK ** -0.5)
    k = proj(A_K, A_V)
    qd_ref[...] = (q * jnp.exp(b)).astype(BF16)
    ki_ref[...] = (k * jnp.exp(-b)).astype(BF16)
    for c in range(tm // CHUNK):
        r = slice(c * CHUNK, (c + 1) * CHUNK)
        last = b[c * CHUNK + CHUNK - 1:(c + 1) * CHUNK]
        ks_ref[r, :] = (k[r] * jnp.exp(last - b[r])).astype(BF16)
        dec_ref[c:c + 1, :] = jnp.exp(last)


def _gdn_in_kernel(x_ref, mod_ref, nw_ref, w_ref, cw_ref, avec_ref, dtvec_ref, tri_ref,
                   q_ref, k_ref, v_ref, g_ref, sm_ref, smt_ref, prev_ref):
    tm = x_ref.shape[0]

    @pl.when(pl.program_id(1) == 0)
    def _():
        prev_ref[...] = jnp.zeros_like(prev_ref)

    h = _modulated_norm(x_ref[...], mod_ref[0], nw_ref[...], 0).astype(BF16)

    def proj(lo, hi):
        return _dot(h, w_ref[:, lo:hi])

    g_ref[...] = _silu(proj(B_G, B_SM)).astype(BF16)

    outs = (q_ref, k_ref, v_ref)
    for grp in range(3):
        cols = slice(grp * GDN_QK, (grp + 1) * GDN_QK)
        p = proj(grp * GDN_QK, (grp + 1) * GDN_QK)
        ext = jnp.concatenate([prev_ref[:, cols], p], axis=0)
        cw = cw_ref[:, cols]
        y = p * cw[CONV_K - 1:CONV_K]
        for s in range(1, CONV_K):
            y = y + pltpu.roll(ext, s, 0)[SUBLANES:] * cw[CONV_K - 1 - s:CONV_K - s]
        prev_ref[:, cols] = p[tm - SUBLANES:tm]
        y = _silu(y)
        if grp == 2:
            outs[grp][...] = y.astype(BF16)
        else:
            mult = GDN_DK ** -0.5 if grp == 0 else 1.0
            for hh in range(GDN_HEADS):
                hc = slice(hh * GDN_DK, (hh + 1) * GDN_DK)
                yh = y[:, hc]
                inv = lax.rsqrt(jnp.sum(yh * yh, axis=-1, keepdims=True) + EPS)
                outs[grp][:, hc] = (yh * (inv * mult)).astype(BF16)

    ps = proj(B_SM, B_END)
    lr = lax.broadcasted_iota(jnp.int32, (CHUNK, LANES), 1)
    beta = _sigmoid(ps)
    g = -avec_ref[...] * _softplus(ps + dtvec_ref[...])
    gc = _split_dot(tri_ref[...], g)
    pieces = []
    for c in range(tm // CHUNK):
        r = slice(c * CHUNK, (c + 1) * CHUNK)
        gcc = gc[r]
        last = gc[c * CHUNK + CHUNK - 1:(c + 1) * CHUNK]
        piece = jnp.where(lr < SM_GC, beta[r],
                jnp.where(lr < SM_EGC, gcc,
                jnp.where(lr < SM_EKL, jnp.exp(gcc),
                jnp.where(lr < SM_DEC, jnp.exp(last - gcc), jnp.exp(jnp.broadcast_to(last, gcc.shape))))))
        pieces.append(piece)
    small = jnp.concatenate(pieces, axis=0)
    sm_ref[...] = small
    smt_ref[...] = small.T


def _mix_kernel(qd_ref, ki_ref, ks_ref, va_ref, ga_ref, dec_ref,
                q_ref, k_ref, v_ref, gd_ref, sm_ref, smt_ref, na_ref, nd_ref,
                oa_ref, od_ref, sa_ref, sd_ref):
    nsub = GROUP // CHUNK

    @pl.when(pl.program_id(1) == 0)
    def _():
        sa_ref[...] = jnp.zeros_like(sa_ref)
        sd_ref[...] = jnp.zeros_like(sd_ref)

    row = lax.broadcasted_iota(jnp.int32, (GROUP, GROUP), 0)
    col = lax.broadcasted_iota(jnp.int32, (GROUP, GROUP), 1)
    log_chunk = CHUNK.bit_length() - 1
    same = (row >> log_chunk) == (col >> log_chunk)
    causal = same & (col <= row)
    strict = same & (col < row)
    eye = (row == col).astype(F32)
    level_masks = []
    for lg in range(log_chunk):
        level_masks.append(((row >> (lg + 1)) == (col >> (lg + 1)))
                           & (((row >> lg) & 1) == 1) & (((col >> lg) & 1) == 0))

    for h in range(GLA_HEADS):
        kc = slice(h * GLA_DK, (h + 1) * GLA_DK)
        vc = slice(h * GLA_DV, (h + 1) * GLA_DV)
        qd = qd_ref[:, kc]
        ki = ki_ref[:, kc]
        ks = ks_ref[:, kc]
        v = va_ref[:, vc]
        att = jnp.where(causal, _dot_nt(qd, ki), 0.0).astype(BF16)
        o = _dot(att, v)
        st = sa_ref[h]
        o_parts = []
        for c in range(nsub):
            r = slice(c * CHUNK, (c + 1) * CHUNK)
            o_parts.append(_dot_nt(qd[r], st.astype(BF16)))
            st = st * dec_ref[0, c:c + 1, kc] + _dot_tn(v[r], ks[r])
        sa_ref[h] = st
        o = o + jnp.concatenate(o_parts, axis=0)
        o = _rms(o, na_ref[...]) * ga_ref[:, vc].astype(F32)
        oa_ref[:, vc] = o.astype(BF16)

    sm = sm_ref[...]
    for h in range(GDN_HEADS):
        hc = slice(h * GDN_DK, (h + 1) * GDN_DK)
        beta = sm[:, SM_BETA + h:SM_BETA + h + 1]
        gc_c = sm[:, SM_GC + h:SM_GC + h + 1]
        egc = sm[:, SM_EGC + h:SM_EGC + h + 1]
        ekl = sm[:, SM_EKL + h:SM_EKL + h + 1]
        gc_r = smt_ref[SM_GC + h:SM_GC + h + 1, :]
        q = q_ref[:, hc]
        k = k_ref[:, hc]
        kf = k.astype(F32)
        kb = kf * beta
        decay = jnp.exp(jnp.where(causal, gc_c - gc_r, NEG_BIG))
        a_mat = jnp.where(strict, _dot_nt(kb.astype(BF16), k) * decay, 0.0)
        att = (_dot_nt(q, k) * decay).astype(BF16)

        tinv = eye
        for lvl in level_masks:
            tb = tinv.astype(BF16)
            x_mat = _dot(jnp.where(lvl, a_mat, 0.0).astype(BF16), tb)
            tinv = tinv - _dot(tb, x_mat.astype(BF16))

        rhs = jnp.concatenate([(v_ref[:, hc].astype(F32) * beta).astype(BF16),
                               (kb * egc).astype(BF16)], axis=1)
        sol = _dot(tinv.astype(BF16), rhs)
        u = sol[:, :GDN_DV]
        w = sol[:, GDN_DV:].astype(BF16)
        qd = (q.astype(F32) * egc).astype(BF16)
        kst = (kf * ekl).astype(BF16)

        st = sd_ref[h]
        o_parts, vn_parts = [], []
        for c in range(nsub):
            r = slice(c * CHUNK, (c + 1) * CHUNK)
            sb = st.astype(BF16)
            vn = u[r] - _dot(w[r], sb)
            o_parts.append(_dot(qd[r], sb))
            dec = sm[(c + 1) * CHUNK - 1:(c + 1) * CHUNK, SM_DEC + h:SM_DEC + h + 1]
            st = st * dec + _dot_tn(kst[r], vn.astype(BF16))
            vn_parts.append(vn)
        sd_ref[h] = st
        vn_all = jnp.concatenate(vn_parts, axis=0).astype(BF16)
        o = jnp.concatenate(o_parts, axis=0) + _dot(att, vn_all)
        o = _rms(o, nd_ref[...]) * gd_ref[:, hc].astype(F32)
        od_ref[:, hc] = o.astype(BF16)


def _merge_kernel(oa_ref, od_ref, ma_ref, md_ref, x_ref, mod_ref, nw_ref,
                  wa_ref, wd_ref, wo_ref, o_ref):
    ya = _dot(oa_ref[...], wa_ref[...])
    yd = _dot(od_ref[...], wd_ref[...])
    merged = ma_ref[...].astype(F32) * ya + md_ref[...].astype(F32) * yd
    z = _dot(merged.astype(BF16), wo_ref[...])
    gate = mod_ref[0][2:3]
    o_ref[...] = x_ref[...] + gate * _rms(z, nw_ref[...])


def _mlp_kernel(x_ref, mod_ref, nw1_ref, nw2_ref, w1_ref, w2_ref, o_ref):
    x = x_ref[...]
    mod = mod_ref[0]
    h = _modulated_norm(x, mod, nw1_ref[...], 3).astype(BF16)
    ff_tile = 1024
    y = jnp.zeros(x.shape, F32)
    for j in range(D_FF // ff_tile):
        fc = slice(j * ff_tile, (j + 1) * ff_tile)
        a = jnp.maximum(_dot(h, w1_ref[:, fc]), 0.0)
        y = y + _dot((a * a).astype(BF16), w2_ref[fc, :])
    o_ref[...] = x + mod[5:6] * _rms(y, nw2_ref[...])


def _const_spec(shape):
    nd = len(shape)
    return pl.BlockSpec(shape, lambda *_: (0,) * nd, pipeline_mode=pl.Buffered(1))


def _params(sem):
    return pltpu.CompilerParams(dimension_semantics=sem, vmem_limit_bytes=VMEM_LIMIT)


def kernel(x, c, ada_w, ada_b, pre_mix_w, post_mix_w, pre_mlp_w, post_mlp_w, w_in, gla_w_lr, gla_b_lr, gla_onorm_w, gdn_conv_w, gdn_a_log, gdn_dt_bias, gdn_onorm_w, w_branch_gla, w_branch_gdn, w_out, mlp_w1, mlp_w2):
    bsz, seq, d = x.shape
    depth = ada_w.shape[0]
    assert d == D_MODEL and seq % TOKEN_TILE == 0 and seq % GROUP == 0
    tm = TOKEN_TILE
    nt = seq // tm
    tokens = bsz * seq
    ng = seq // GROUP

    ri = lax.broadcasted_iota(jnp.int32, (tm, tm), 0)
    ci = lax.broadcasted_iota(jnp.int32, (tm, tm), 1)
    tri = ((ri // CHUNK == ci // CHUNK) & (ci <= ri)).astype(BF16)

    xf = x.reshape(tokens, d)
    c_pad = jnp.pad(c, ((0, SUBLANES - bsz), (0, 0)))
    row = lambda v: v.reshape(1, -1)

    tile_map = lambda b, i: (b * nt + i, 0)
    mod_spec = pl.BlockSpec((1, 6, d), lambda b, i: (b, 0, 0))

    def tile_spec(width):
        return pl.BlockSpec((tm, width), tile_map)

    for l in range(depth):
        mod = _ada(c_pad, ada_w[l], row(ada_b[l]))[:bsz].reshape(bsz, 6, d)

        wi = w_in[l]
        o = [0]
        def take(n):
            s = wi[:, o[0]:o[0] + n]
            o[0] += n
            return s
        a_q, a_k, a_v, a_g, a_lr = take(GLA_QK), take(GLA_QK), take(GLA_V), take(GLA_V), take(GLA_RANK)
        d_qkv, d_g, d_beta, d_a = take(3 * GDN_QK), take(GDN_V), take(GDN_HEADS), take(GDN_HEADS)
        m_a, m_d = take(d), take(d)
        lr_pad = jnp.pad(a_lr, ((0, 0), (0, LANES - GLA_RANK)))
        w_a = jnp.concatenate([a_q, a_k, a_v, a_g, m_a, m_d, lr_pad], axis=1).astype(BF16)
        sm_cols = jnp.concatenate([d_beta] + [d_a] * 4, axis=1)
        sm_pad = jnp.pad(sm_cols, ((0, 0), (0, LANES - sm_cols.shape[1])))
        w_b = jnp.concatenate([d_qkv, d_g, sm_pad], axis=1).astype(BF16)
        wlr = jnp.pad(gla_w_lr[l], ((0, LANES - GLA_RANK), (0, 0))).astype(BF16)
        lane_pad = (SM_GC, LANES - SM_GC - 4 * GDN_HEADS)
        avec = row(jnp.pad(jnp.tile(jnp.exp(gdn_a_log[l]), 4), lane_pad))
        dtvec = row(jnp.pad(jnp.tile(gdn_dt_bias[l], 4), lane_pad))

        qd, ki, ks, va, ga, ma, md, dec = pl.pallas_call(
            _gla_in_kernel,
            grid=(bsz, nt),
            in_specs=[tile_spec(d), mod_spec, _const_spec((1, d)), _const_spec((d, A_END)),
                      _const_spec((LANES, GLA_QK)), _const_spec((1, GLA_QK)), _const_spec((tm, tm))],
            out_specs=[tile_spec(GLA_QK)] * 3 + [tile_spec(GLA_V)] * 2 + [tile_spec(d)] * 2
                      + [pl.BlockSpec((tm // CHUNK, GLA_QK), tile_map)],
            out_shape=[jax.ShapeDtypeStruct((tokens, GLA_QK), BF16)] * 3
                      + [jax.ShapeDtypeStruct((tokens, GLA_V), BF16)] * 2
                      + [jax.ShapeDtypeStruct((tokens, d), BF16)] * 2
                      + [jax.ShapeDtypeStruct((tokens // CHUNK, GLA_QK), F32)],
            compiler_params=_params(("parallel", "parallel")),
            name="gla_in",
        )(xf, mod, row(pre_mix_w[l]), w_a, wlr, row(gla_b_lr[l]), tri)

        qg, kg, vg, gd, sm, smt = pl.pallas_call(
            _gdn_in_kernel,
            grid=(bsz, nt),
            in_specs=[tile_spec(d), mod_spec, _const_spec((1, d)), _const_spec((d, B_END)),
                      _const_spec((CONV_K, 3 * GDN_QK)), _const_spec((1, LANES)), _const_spec((1, LANES)),
                      _const_spec((tm, tm))],
            out_specs=[tile_spec(GDN_QK)] * 4 + [tile_spec(LANES),
                       pl.BlockSpec((LANES, tm), lambda b, i: (0, b * nt + i))],
            out_shape=[jax.ShapeDtypeStruct((tokens, GDN_QK), BF16)] * 4
                      + [jax.ShapeDtypeStruct((tokens, LANES), F32),
                         jax.ShapeDtypeStruct((LANES, tokens), F32)],
            scratch_shapes=[pltpu.VMEM((SUBLANES, 3 * GDN_QK), F32)],
            compiler_params=_params(("arbitrary", "arbitrary")),
            name="gdn_in",
        )(xf, mod, row(pre_mix_w[l]), w_b, gdn_conv_w[l], avec, dtvec, tri)

        nsub = GROUP // CHUNK
        grp_map = lambda b, i: (b * ng + i, 0)

        def grp_spec(width):
            return pl.BlockSpec((GROUP, width), grp_map)

        oa, od = pl.pallas_call(
            _mix_kernel,
            grid=(bsz, ng),
            in_specs=[grp_spec(GLA_QK)] * 3 + [grp_spec(GLA_V)] * 2
                     + [pl.BlockSpec((1, nsub, GLA_QK), lambda b, i: (b * ng + i, 0, 0))]
                     + [grp_spec(GDN_QK)] * 4 + [grp_spec(LANES),
                        pl.BlockSpec((LANES, GROUP), lambda b, i: (0, b * ng + i)),
                        _const_spec((1, GLA_DV)), _const_spec((1, GDN_DV))],
            out_specs=[grp_spec(GLA_V), grp_spec(GDN_V)],
            out_shape=[jax.ShapeDtypeStruct((tokens, GLA_V), BF16),
                       jax.ShapeDtypeStruct((tokens, GDN_V), BF16)],
            scratch_shapes=[pltpu.VMEM((GLA_HEADS, GLA_DV, GLA_DK), F32),
                            pltpu.VMEM((GDN_HEADS, GDN_DK, GDN_DV), F32)],
            compiler_params=_params(("arbitrary", "arbitrary")),
            name="mix",
        )(qd, ki, ks, va, ga, dec.reshape(tokens // GROUP, nsub, GLA_QK),
          qg, kg, vg, gd, sm, smt, row(gla_onorm_w[l]), row(gdn_onorm_w[l]))

        x1 = pl.pallas_call(
            _merge_kernel,
            grid=(bsz, nt),
            in_specs=[tile_spec(d)] * 5 + [mod_spec, _const_spec((1, d))] + [_const_spec((d, d))] * 3,
            out_specs=tile_spec(d),
            out_shape=jax.ShapeDtypeStruct((tokens, d), F32),
            compiler_params=_params(("parallel", "parallel")),
            name="merge",
        )(oa, od, ma, md, xf, mod, row(post_mix_w[l]),
          w_branch_gla[l].astype(BF16), w_branch_gdn[l].astype(BF16), w_out[l].astype(BF16))

        xf = pl.pallas_call(
            _mlp_kernel,
            grid=(bsz, nt),
            in_specs=[tile_spec(d), mod_spec, _const_spec((1, d)), _const_spec((1, d)),
                      _const_spec((d, D_FF)), _const_spec((D_FF, d))],
            out_specs=tile_spec(d),
            out_shape=jax.ShapeDtypeStruct((tokens, d), F32),
            compiler_params=_params(("parallel", "parallel")),
            name="mlp",
        )(x1, mod, row(pre_mlp_w[l]), row(post_mlp_w[l]), mlp_w1[l].astype(BF16), mlp_w2[l].astype(BF16))

    return xf.reshape(bsz, seq, d)
```

```python
import functools

import jax
import jax.numpy as jnp
from jax import lax
from jax.experimental import pallas as pl
from jax.experimental.pallas import tpu as pltpu

F32 = jnp.float32
BF16 = jnp.bfloat16

D_MODEL = 1024
GLA_HEADS, GLA_DK, GLA_DV, GLA_RANK = 4, 128, 256, 16
GLA_GATE_NORM = 16.0
GDN_HEADS, GDN_DK, GDN_DV = 8, 128, 128
CONV_K = 4
CHUNK = 64
D_FF = 4 * D_MODEL
EPS = 1e-6
GLA_QK = GLA_HEADS * GLA_DK
GLA_V = GLA_HEADS * GLA_DV
GDN_QK = GDN_HEADS * GDN_DK
GDN_V = GDN_HEADS * GDN_DV

LANES = 128
SUBLANES = 8
VMEM_LIMIT = 56 * 1024 * 1024

TOKEN_TILE = 512
GROUP = 256
NEG_BIG = -1e30

A_Q, A_K, A_V, A_G, A_MA, A_MD, A_LR, A_END = 0, 512, 1024, 2048, 3072, 4096, 5120, 5248
B_Q, B_K, B_V, B_G, B_SM, B_END = 0, 1024, 2048, 3072, 4096, 4224
SM_BETA, SM_GC, SM_EGC, SM_EKL, SM_DEC = 0, 8, 16, 24, 32


def _dot(a, b):
    return jnp.dot(a, b, preferred_element_type=F32)


def _dot_nt(a, b):
    return lax.dot_general(a, b, (((1,), (1,)), ((), ())), preferred_element_type=F32)


def _dot_tn(a, b):
    return lax.dot_general(a, b, (((0,), (0,)), ((), ())), preferred_element_type=F32)


def _sigmoid(x):
    return 1.0 / (1.0 + jnp.exp(-x))


def _silu(x):
    return x * _sigmoid(x)


def _softplus(x):
    return jnp.maximum(x, 0.0) + jnp.log(1.0 + jnp.exp(-jnp.abs(x)))


def _rms(x, w):
    return x * lax.rsqrt(jnp.mean(x * x, axis=-1, keepdims=True) + EPS) * w


def _split_dot(tri, x):
    hi = x.astype(BF16)
    lo = (x - hi.astype(F32)).astype(BF16)
    return _dot(tri, hi) + _dot(tri, lo)


def _ada_kernel(c_ref, w_ref, b_ref, o_ref):
    c = c_ref[...]
    o_ref[...] = _dot(_silu(c).astype(BF16), w_ref[...].astype(BF16)) + b_ref[...]


def _ada(c_pad, ada_w, ada_b):
    n = ada_w.shape[1]
    tn = 1024
    return pl.pallas_call(
        _ada_kernel,
        grid=(n // tn,),
        in_specs=[pl.BlockSpec((SUBLANES, D_MODEL), lambda j: (0, 0)),
                  pl.BlockSpec((D_MODEL, tn), lambda j: (0, j)),
                  pl.BlockSpec((1, tn), lambda j: (0, j))],
        out_specs=pl.BlockSpec((SUBLANES, tn), lambda j: (0, j)),
        out_shape=jax.ShapeDtypeStruct((SUBLANES, n), F32),
        name="ada",
    )(c_pad, ada_w, ada_b)


def _modulated_norm(x, mod, nw, first):
    shift = mod[first:first + 1]
    scale = mod[first + 1:first + 2]
    return _rms(x, nw) * (1.0 + scale) + shift


def _gla_in_kernel(x_ref, mod_ref, nw_ref, w_ref, wlr_ref, blr_ref, tri_ref,
                   qd_ref, ki_ref, ks_ref, v_ref, g_ref, ma_ref, md_ref, dec_ref):
    tm = x_ref.shape[0]
    h = _modulated_norm(x_ref[...], mod_ref[0], nw_ref[...], 0).astype(BF16)

    def proj(lo, hi):
        return _dot(h, w_ref[:, lo:hi])

    v_ref[...] = proj(A_V, A_G).astype(BF16)
    g_ref[...] = _silu(proj(A_G, A_MA)).astype(BF16)
    ma_ref[...] = _sigmoid(proj(A_MA, A_MD)).astype(BF16)
    md_ref[...] = _sigmoid(proj(A_MD, A_LR)).astype(BF16)

    z = _dot(proj(A_LR, A_END).astype(BF16), wlr_ref[...]) + blr_ref[...]
    gk = (jnp.minimum(z, 0.0) - jnp.log(1.0 + jnp.exp(-jnp.abs(z)))) * (1.0 / GLA_GATE_NORM)
    b = _split_dot(tri_ref[...], gk)
    q = proj(A_Q, A_K) * (GLA_DK ** -0.5)
    k = proj(A_K, A_V)
    qd_ref[...] = (q * jnp.exp(b)).astype(BF16)
    ki_ref[...] = (k * jnp.exp(-b)).astype(BF16)
    for c in range(tm // CHUNK):
        r = slice(c * CHUNK, (c + 1) * CHUNK)
        last = b[c * CHUNK + CHUNK - 1:(c + 1) * CHUNK]
        ks_ref[r, :] = (k[r] * jnp.exp(last - b[r])).astype(BF16)
        dec_ref[c:c + 1, :] = jnp.exp(last)


def _gdn_in_kernel(x_ref, mod_ref, nw_ref, w_ref, cw_ref, avec_ref, dtvec_ref, tri_ref,
                   q_ref, k_ref, v_ref, g_ref, sm_ref, smt_ref, prev_ref):
    tm = x_ref.shape[0]

    @pl.when(pl.program_id(1) == 0)
    def _():
        prev_ref[...] = jnp.zeros_like(prev_ref)

    h = _modulated_norm(x_ref[...], mod_ref[0], nw_ref[...], 0).astype(BF16)

    def proj(lo, hi):
        return _dot(h, w_ref[:, lo:hi])

    g_ref[...] = _silu(proj(B_G, B_SM)).astype(BF16)

    outs = (q_ref, k_ref, v_ref)
    for grp in range(3):
        cols = slice(grp * GDN_QK, (grp + 1) * GDN_QK)
        p = proj(grp * GDN_QK, (grp + 1) * GDN_QK)
        ext = jnp.concatenate([prev_ref[:, cols], p], axis=0)
        cw = cw_ref[:, cols]
        y = p * cw[CONV_K - 1:CONV_K]
        for s in range(1, CONV_K):
            y = y + pltpu.roll(ext, s, 0)[SUBLANES:] * cw[CONV_K - 1 - s:CONV_K - s]
        prev_ref[:, cols] = p[tm - SUBLANES:tm]
        y = _silu(y)
        if grp == 2:
            outs[grp][...] = y.astype(BF16)
        else:
            mult = GDN_DK ** -0.5 if grp == 0 else 1.0
            for hh in range(GDN_HEADS):
                hc = slice(hh * GDN_DK, (hh + 1) * GDN_DK)
                yh = y[:, hc]
                inv = lax.rsqrt(jnp.sum(yh * yh, axis=-1, keepdims=True) + EPS)
                outs[grp][:, hc] = (yh * (inv * mult)).astype(BF16)

    ps = proj(B_SM, B_END)
    lr = lax.broadcasted_iota(jnp.int32, (CHUNK, LANES), 1)
    beta = _sigmoid(ps)
    g = -avec_ref[...] * _softplus(ps + dtvec_ref[...])
    gc = _split_dot(tri_ref[...], g)
    pieces = []
    for c in range(tm // CHUNK):
        r = slice(c * CHUNK, (c + 1) * CHUNK)
        gcc = gc[r]
        last = gc[c * CHUNK + CHUNK - 1:(c + 1) * CHUNK]
        piece = jnp.where(lr < SM_GC, beta[r],
                jnp.where(lr < SM_EGC, gcc,
                jnp.where(lr < SM_EKL, jnp.exp(gcc),
                jnp.where(lr < SM_DEC, jnp.exp(last - gcc), jnp.exp(jnp.broadcast_to(last, gcc.shape))))))
        pieces.append(piece)
    small = jnp.concatenate(pieces, axis=0)
    sm_ref[...] = small
    smt_ref[...] = small.T


def _mix_kernel(qd_ref, ki_ref, ks_ref, va_ref, ga_ref, dec_ref,
                q_ref, k_ref, v_ref, gd_ref, sm_ref, smt_ref, na_ref, nd_ref,
                oa_ref, od_ref, sa_ref, sd_ref):
    nsub = GROUP // CHUNK

    @pl.when(pl.program_id(1) == 0)
    def _():
        sa_ref[...] = jnp.zeros_like(sa_ref)
        sd_ref[...] = jnp.zeros_like(sd_ref)

    row = lax.broadcasted_iota(jnp.int32, (GROUP, GROUP), 0)
    col = lax.broadcasted_iota(jnp.int32, (GROUP, GROUP), 1)
    log_chunk = CHUNK.bit_length() - 1
    same = (row >> log_chunk) == (col >> log_chunk)
    causal = same & (col <= row)
    strict = same & (col < row)
    eye = (row == col).astype(F32)
    level_masks = []
    for lg in range(log_chunk):
        level_masks.append(((row >> (lg + 1)) == (col >> (lg + 1)))
                           & (((row >> lg) & 1) == 1) & (((col >> lg) & 1) == 0))

    a_heads = range(GLA_HEADS)
    kcs = [slice(h * GLA_DK, (h + 1) * GLA_DK) for h in a_heads]
    vcs = [slice(h * GLA_DV, (h + 1) * GLA_DV) for h in a_heads]
    qd_a = [qd_ref[:, kcs[h]] for h in a_heads]
    ks_a = [ks_ref[:, kcs[h]] for h in a_heads]
    v_a = [va_ref[:, vcs[h]] for h in a_heads]
    att_a = [jnp.where(causal, _dot_nt(qd_a[h], ki_ref[:, kcs[h]]), 0.0).astype(BF16) for h in a_heads]
    o_a = [_dot(att_a[h], v_a[h]) for h in a_heads]
    st_a = [sa_ref[h] for h in a_heads]
    oi_a = [[] for _ in a_heads]

    sm = sm_ref[...]
    d_heads = range(GDN_HEADS)
    hcs = [slice(h * GDN_DK, (h + 1) * GDN_DK) for h in d_heads]
    a_mat, att_d, rhs, qd_d, kst_d = [], [], [], [], []
    for h in d_heads:
        beta = sm[:, SM_BETA + h:SM_BETA + h + 1]
        gc_c = sm[:, SM_GC + h:SM_GC + h + 1]
        egc = sm[:, SM_EGC + h:SM_EGC + h + 1]
        ekl = sm[:, SM_EKL + h:SM_EKL + h + 1]
        gc_r = smt_ref[SM_GC + h:SM_GC + h + 1, :]
        q = q_ref[:, hcs[h]]
        k = k_ref[:, hcs[h]]
        kf = k.astype(F32)
        kb = kf * beta
        decay = jnp.exp(jnp.where(causal, gc_c - gc_r, NEG_BIG))
        a_mat.append(jnp.where(strict, _dot_nt(kb.astype(BF16), k) * decay, 0.0))
        att_d.append((_dot_nt(q, k) * decay).astype(BF16))
        rhs.append(jnp.concatenate([(v_ref[:, hcs[h]].astype(F32) * beta).astype(BF16),
                                    (kb * egc).astype(BF16)], axis=1))
        qd_d.append((q.astype(F32) * egc).astype(BF16))
        kst_d.append((kf * ekl).astype(BF16))

    tinv = [eye - jnp.where(level_masks[0], a_mat[h], 0.0) for h in d_heads]
    for lvl in level_masks[1:]:
        tb = [tinv[h].astype(BF16) for h in d_heads]
        x_mat = [_dot(jnp.where(lvl, a_mat[h], 0.0).astype(BF16), tb[h]).astype(BF16) for h in d_heads]
        tinv = [tinv[h] - _dot(tb[h], x_mat[h]) for h in d_heads]
    sol = [_dot(tinv[h].astype(BF16), rhs[h]) for h in d_heads]
    u_d = [sol[h][:, :GDN_DV] for h in d_heads]
    w_d = [sol[h][:, GDN_DV:].astype(BF16) for h in d_heads]

    st_d = [sd_ref[h] for h in d_heads]
    oi_d = [[] for _ in d_heads]
    vn_d = [[] for _ in d_heads]
    for c in range(nsub):
        r = slice(c * CHUNK, (c + 1) * CHUNK)
        sb = [st_d[h].astype(BF16) for h in d_heads]
        vn = [u_d[h][r] - _dot(w_d[h][r], sb[h]) for h in d_heads]
        for h in a_heads:
            oi_a[h].append(_dot_nt(qd_a[h][r], st_a[h].astype(BF16)))
            st_a[h] = st_a[h] * dec_ref[0, c:c + 1, kcs[h]] + _dot_tn(v_a[h][r], ks_a[h][r])
        for h in d_heads:
            oi_d[h].append(_dot(qd_d[h][r], sb[h]))
        for h in d_heads:
            dec = sm[(c + 1) * CHUNK - 1:(c + 1) * CHUNK, SM_DEC + h:SM_DEC + h + 1]
            st_d[h] = st_d[h] * dec + _dot_tn(kst_d[h][r], vn[h].astype(BF16))
            vn_d[h].append(vn[h])

    for h in a_heads:
        sa_ref[h] = st_a[h]
        o = o_a[h] + jnp.concatenate(oi_a[h], axis=0)
        o = _rms(o, na_ref[...]) * ga_ref[:, vcs[h]].astype(F32)
        oa_ref[:, vcs[h]] = o.astype(BF16)
    for h in d_heads:
        sd_ref[h] = st_d[h]
        vn_all = jnp.concatenate(vn_d[h], axis=0).astype(BF16)
        o = jnp.concatenate(oi_d[h], axis=0) + _dot(att_d[h], vn_all)
        o = _rms(o, nd_ref[...]) * gd_ref[:, hcs[h]].astype(F32)
        od_ref[:, hcs[h]] = o.astype(BF16)


def _merge_kernel(oa_ref, od_ref, ma_ref, md_ref, x_ref, mod_ref, nw_ref,
                  wa_ref, wd_ref, wo_ref, o_ref):
    ya = _dot(oa_ref[...], wa_ref[...])
    yd = _dot(od_ref[...], wd_ref[...])
    merged = ma_ref[...].astype(F32) * ya + md_ref[...].astype(F32) * yd
    z = _dot(merged.astype(BF16), wo_ref[...])
    gate = mod_ref[0][2:3]
    o_ref[...] = x_ref[...] + gate * _rms(z, nw_ref[...])


def _mlp_kernel(x_ref, mod_ref, nw1_ref, nw2_ref, w1_ref, w2_ref, o_ref):
    x = x_ref[...]
    mod = mod_ref[0]
    h = _modulated_norm(x, mod, nw1_ref[...], 3).astype(BF16)
    ff_tile = 1024
    y = jnp.zeros(x.shape, F32)
    for j in range(D_FF // ff_tile):
        fc = slice(j * ff_tile, (j + 1) * ff_tile)
        a = jnp.maximum(_dot(h, w1_ref[:, fc]), 0.0)
        y = y + _dot((a * a).astype(BF16), w2_ref[fc, :])
    o_ref[...] = x + mod[5:6] * _rms(y, nw2_ref[...])


def _const_spec(shape):
    nd = len(shape)
    return pl.BlockSpec(shape, lambda *_: (0,) * nd, pipeline_mode=pl.Buffered(1))


def _params(sem):
    return pltpu.CompilerParams(dimension_semantics=sem, vmem_limit_bytes=VMEM_LIMIT)


def kernel(x, c, ada_w, ada_b, pre_mix_w, post_mix_w, pre_mlp_w, post_mlp_w, w_in, gla_w_lr, gla_b_lr, gla_onorm_w, gdn_conv_w, gdn_a_log, gdn_dt_bias, gdn_onorm_w, w_branch_gla, w_branch_gdn, w_out, mlp_w1, mlp_w2):
    bsz, seq, d = x.shape
    depth = ada_w.shape[0]
    assert d == D_MODEL and seq % TOKEN_TILE == 0 and seq % GROUP == 0
    tm = TOKEN_TILE
    nt = seq // tm
    tokens = bsz * seq
    ng = seq // GROUP

    ri = lax.broadcasted_iota(jnp.int32, (tm, tm), 0)
    ci = lax.broadcasted_iota(jnp.int32, (tm, tm), 1)
    tri = ((ri // CHUNK == ci // CHUNK) & (ci <= ri)).astype(BF16)

    xf = x.reshape(tokens, d)
    c_pad = jnp.pad(c, ((0, SUBLANES - bsz), (0, 0)))
    row = lambda v: v.reshape(1, -1)

    tile_map = lambda b, i: (b * nt + i, 0)
    mod_spec = pl.BlockSpec((1, 6, d), lambda b, i: (b, 0, 0))

    def tile_spec(width):
        return pl.BlockSpec((tm, width), tile_map)

    for l in range(depth):
        mod = _ada(c_pad, ada_w[l], row(ada_b[l]))[:bsz].reshape(bsz, 6, d)

        wi = w_in[l]
        o = [0]
        def take(n):
            s = wi[:, o[0]:o[0] + n]
            o[0] += n
            return s
        a_q, a_k, a_v, a_g, a_lr = take(GLA_QK), take(GLA_QK), take(GLA_V), take(GLA_V), take(GLA_RANK)
        d_qkv, d_g, d_beta, d_a = take(3 * GDN_QK), take(GDN_V), take(GDN_HEADS), take(GDN_HEADS)
        m_a, m_d = take(d), take(d)
        lr_pad = jnp.pad(a_lr, ((0, 0), (0, LANES - GLA_RANK)))
        w_a = jnp.concatenate([a_q, a_k, a_v, a_g, m_a, m_d, lr_pad], axis=1).astype(BF16)
        sm_cols = jnp.concatenate([d_beta] + [d_a] * 4, axis=1)
        sm_pad = jnp.pad(sm_cols, ((0, 0), (0, LANES - sm_cols.shape[1])))
        w_b = jnp.concatenate([d_qkv, d_g, sm_pad], axis=1).astype(BF16)
        wlr = jnp.pad(gla_w_lr[l], ((0, LANES - GLA_RANK), (0, 0))).astype(BF16)
        lane_pad = (SM_GC, LANES - SM_GC - 4 * GDN_HEADS)
        avec = row(jnp.pad(jnp.tile(jnp.exp(gdn_a_log[l]), 4), lane_pad))
        dtvec = row(jnp.pad(jnp.tile(gdn_dt_bias[l], 4), lane_pad))

        qd, ki, ks, va, ga, ma, md, dec = pl.pallas_call(
            _gla_in_kernel,
            grid=(bsz, nt),
            in_specs=[tile_spec(d), mod_spec, _const_spec((1, d)), _const_spec((d, A_END)),
                      _const_spec((LANES, GLA_QK)), _const_spec((1, GLA_QK)), _const_spec((tm, tm))],
            out_specs=[tile_spec(GLA_QK)] * 3 + [tile_spec(GLA_V)] * 2 + [tile_spec(d)] * 2
                      + [pl.BlockSpec((tm // CHUNK, GLA_QK), tile_map)],
            out_shape=[jax.ShapeDtypeStruct((tokens, GLA_QK), BF16)] * 3
                      + [jax.ShapeDtypeStruct((tokens, GLA_V), BF16)] * 2
                      + [jax.ShapeDtypeStruct((tokens, d), BF16)] * 2
                      + [jax.ShapeDtypeStruct((tokens // CHUNK, GLA_QK), F32)],
            compiler_params=_params(("parallel", "parallel")),
            name="gla_in",
        )(xf, mod, row(pre_mix_w[l]), w_a, wlr, row(gla_b_lr[l]), tri)

        qg, kg, vg, gd, sm, smt = pl.pallas_call(
            _gdn_in_kernel,
            grid=(bsz, nt),
            in_specs=[tile_spec(d), mod_spec, _const_spec((1, d)), _const_spec((d, B_END)),
                      _const_spec((CONV_K, 3 * GDN_QK)), _const_spec((1, LANES)), _const_spec((1, LANES)),
                      _const_spec((tm, tm))],
            out_specs=[tile_spec(GDN_QK)] * 4 + [tile_spec(LANES),
                       pl.BlockSpec((LANES, tm), lambda b, i: (0, b * nt + i))],
            out_shape=[jax.ShapeDtypeStruct((tokens, GDN_QK), BF16)] * 4
                      + [jax.ShapeDtypeStruct((tokens, LANES), F32),
                         jax.ShapeDtypeStruct((LANES, tokens), F32)],
            scratch_shapes=[pltpu.VMEM((SUBLANES, 3 * GDN_QK), F32)],
            compiler_params=_params(("arbitrary", "arbitrary")),
            name="gdn_in",
        )(xf, mod, row(pre_mix_w[l]), w_b, gdn_conv_w[l], avec, dtvec, tri)

        nsub = GROUP // CHUNK
        grp_map = lambda b, i: (b * ng + i, 0)

        def grp_spec(width):
            return pl.BlockSpec((GROUP, width), grp_map)

        oa, od = pl.pallas_call(
            _mix_kernel,
            grid=(bsz, ng),
            in_specs=[grp_spec(GLA_QK)] * 3 + [grp_spec(GLA_V)] * 2
                     + [pl.BlockSpec((1, nsub, GLA_QK), lambda b, i: (b * ng + i, 0, 0))]
                     + [grp_spec(GDN_QK)] * 4 + [grp_spec(LANES),
                        pl.BlockSpec((LANES, GROUP), lambda b, i: (0, b * ng + i)),
                        _const_spec((1, GLA_DV)), _const_spec((1, GDN_DV))],
            out_specs=[grp_spec(GLA_V), grp_spec(GDN_V)],
            out_shape=[jax.ShapeDtypeStruct((tokens, GLA_V), BF16),
                       jax.ShapeDtypeStruct((tokens, GDN_V), BF16)],
            scratch_shapes=[pltpu.VMEM((GLA_HEADS, GLA_DV, GLA_DK), F32),
                            pltpu.VMEM((GDN_HEADS, GDN_DK, GDN_DV), F32)],
            compiler_params=_params(("arbitrary", "arbitrary")),
            name="mix",
        )(qd, ki, ks, va, ga, dec.reshape(tokens // GROUP, nsub, GLA_QK),
          qg, kg, vg, gd, sm, smt, row(gla_onorm_w[l]), row(gdn_onorm_w[l]))

        x1 = pl.pallas_call(
            _merge_kernel,
            grid=(bsz, nt),
            in_specs=[tile_spec(d)] * 5 + [mod_spec, _const_spec((1, d))] + [_const_spec((d, d))] * 3,
            out_specs=tile_spec(d),
            out_shape=jax.ShapeDtypeStruct((tokens, d), F32),
            compiler_params=_params(("parallel", "parallel")),
            name="merge",
        )(oa, od, ma, md, xf, mod, row(post_mix_w[l]),
          w_branch_gla[l].astype(BF16), w_branch_gdn[l].astype(BF16), w_out[l].astype(BF16))

        xf = pl.pallas_call(
            _mlp_kernel,
            grid=(bsz, nt),
            in_specs=[tile_spec(d), mod_spec, _const_spec((1, d)), _const_spec((1, d)),
                      _const_spec((d, D_FF)), _const_spec((D_FF, d))],
            out_specs=tile_spec(d),
            out_shape=jax.ShapeDtypeStruct((tokens, d), F32),
            compiler_params=_params(("parallel", "parallel")),
            name="mlp",
        )(x1, mod, row(pre_mlp_w[l]), row(post_mlp_w[l]), mlp_w1[l].astype(BF16), mlp_w2[l].astype(BF16))

    return xf.reshape(bsz, seq, d)
```

```python
import jax
import jax.numpy as jnp
from jax import lax
from jax.experimental import pallas as pl
from jax.experimental.pallas import tpu as pltpu

F32 = jnp.float32
BF16 = jnp.bfloat16

D_MODEL = 1024
GLA_HEADS, GLA_DK, GLA_DV, GLA_RANK = 4, 128, 256, 16
GLA_GATE_NORM = 16.0
GDN_HEADS, GDN_DK, GDN_DV = 8, 128, 128
CONV_K = 4
CHUNK = 64
D_FF = 4 * D_MODEL
EPS = 1e-6
GLA_QK = GLA_HEADS * GLA_DK
GLA_V = GLA_HEADS * GLA_DV
GDN_QK = GDN_HEADS * GDN_DK
GDN_V = GDN_HEADS * GDN_DV

LANES = 128
SUBLANES = 8
VMEM_LIMIT = 60 * 1024 * 1024

TOKEN_TILE = 512
GROUP = 256
NSUB = GROUP // CHUNK
NEG_BIG = -1e30

W_AQ, W_AK, W_AV, W_AG, W_MA, W_MD, W_LR = 0, 512, 1024, 2048, 3072, 4096, 5120
W_SM, W_DQ, W_DK, W_DV, W_DG, W_END = 5248, 5376, 6400, 7424, 8448, 9472
SM_BETA, SM_GC, SM_EGC, SM_EKL, SM_DEC = 0, 8, 16, 24, 32


def _dot(a, b):
    return jnp.dot(a, b, preferred_element_type=F32)


def _dot_nt(a, b):
    return lax.dot_general(a, b, (((1,), (1,)), ((), ())), preferred_element_type=F32)


def _dot_tn(a, b):
    return lax.dot_general(a, b, (((0,), (0,)), ((), ())), preferred_element_type=F32)


def _sigmoid(x):
    return 1.0 / (1.0 + jnp.exp(-x))


def _silu(x):
    return x * _sigmoid(x)


def _softplus(x):
    return jnp.maximum(x, 0.0) + jnp.log(1.0 + jnp.exp(-jnp.abs(x)))


def _rms(x, w):
    return x * lax.rsqrt(jnp.mean(x * x, axis=-1, keepdims=True) + EPS) * w


def _split_dot(tri, x):
    hi = x.astype(BF16)
    lo = (x - hi.astype(F32)).astype(BF16)
    return _dot(tri, hi) + _dot(tri, lo)


def _modulated_norm(x, mod, nw, first):
    shift = mod[first:first + 1]
    scale = mod[first + 1:first + 2]
    return _rms(x, nw) * (1.0 + scale) + shift


def _ada_kernel(c_ref, w_ref, b_ref, o_ref):
    c = c_ref[...]
    o_ref[...] = _dot(_silu(c).astype(BF16), w_ref[...].astype(BF16)) + b_ref[...]


def _ada(c_pad, ada_w, ada_b):
    n = ada_w.shape[1]
    tn = 1024
    return pl.pallas_call(
        _ada_kernel,
        grid=(n // tn,),
        in_specs=[pl.BlockSpec((SUBLANES, D_MODEL), lambda j: (0, 0)),
                  pl.BlockSpec((D_MODEL, tn), lambda j: (0, j)),
                  pl.BlockSpec((1, tn), lambda j: (0, j))],
        out_specs=pl.BlockSpec((SUBLANES, tn), lambda j: (0, j)),
        out_shape=jax.ShapeDtypeStruct((SUBLANES, n), F32),
        name="ada",
    )(c_pad, ada_w, ada_b)


def _proj_in_kernel(x_ref, mod_ref, nw_ref, w_ref, wlr_ref, blr_ref, cw_ref, avec_ref, dtvec_ref, tri_ref,
                    qd_ref, ki_ref, ks_ref, va_ref, ga_ref, ma_ref, md_ref, dec_ref,
                    qg_ref, kg_ref, vg_ref, gd_ref, sm_ref, smt_ref, prev_ref):
    tm = x_ref.shape[0]
    nchunk = tm // CHUNK

    @pl.when(pl.program_id(1) == 0)
    def _():
        prev_ref[...] = jnp.zeros_like(prev_ref)

    h = _modulated_norm(x_ref[...], mod_ref[0], nw_ref[...], 0).astype(BF16)

    def proj(lo, hi):
        return _dot(h, w_ref[:, lo:hi])

    p_small = proj(W_LR, W_DQ)
    ps = p_small[:, LANES:]
    lr = lax.broadcasted_iota(jnp.int32, (CHUNK, LANES), 1)
    beta = _sigmoid(ps)
    g = -avec_ref[...] * _softplus(ps + dtvec_ref[...])
    gc = _split_dot(tri_ref[...], g)
    z = _dot(p_small[:, :LANES].astype(BF16), wlr_ref[...]) + blr_ref[...]
    gk = (jnp.minimum(z, 0.0) - jnp.log(1.0 + jnp.exp(-jnp.abs(z)))) * (1.0 / GLA_GATE_NORM)
    b = _split_dot(tri_ref[...], gk)

    pieces = []
    for c in range(nchunk):
        r = slice(c * CHUNK, (c + 1) * CHUNK)
        gcc = gc[r]
        last = gc[c * CHUNK + CHUNK - 1:(c + 1) * CHUNK]
        piece = jnp.where(lr < SM_GC, beta[r],
                jnp.where(lr < SM_EGC, gcc,
                jnp.where(lr < SM_EKL, jnp.exp(gcc),
                jnp.where(lr < SM_DEC, jnp.exp(last - gcc), jnp.exp(jnp.broadcast_to(last, gcc.shape))))))
        pieces.append(piece)
    small = jnp.concatenate(pieces, axis=0)
    sm_ref[...] = small
    smt_ref[...] = small.T

    conv_slab = 2 * GDN_DK
    outs = (qg_ref, kg_ref, vg_ref)

    def conv_item(j):
        grp = (j * conv_slab) // GDN_QK
        cols = slice(j * conv_slab, (j + 1) * conv_slab)
        p = proj(W_DQ + j * conv_slab, W_DQ + (j + 1) * conv_slab)
        ext = jnp.concatenate([prev_ref[:, cols], p], axis=0)
        cw = cw_ref[:, cols]
        y = p * cw[CONV_K - 1:CONV_K]
        for s in range(1, CONV_K):
            y = y + pltpu.roll(ext, s, 0)[SUBLANES:] * cw[CONV_K - 1 - s:CONV_K - s]
        prev_ref[:, cols] = p[tm - SUBLANES:tm]
        y = _silu(y)
        lo = j * conv_slab - grp * GDN_QK
        if grp == 2:
            outs[grp][:, lo:lo + conv_slab] = y.astype(BF16)
        else:
            mult = GDN_DK ** -0.5 if grp == 0 else 1.0
            for hh in range(conv_slab // GDN_DK):
                yh = y[:, hh * GDN_DK:(hh + 1) * GDN_DK]
                inv = lax.rsqrt(jnp.sum(yh * yh, axis=-1, keepdims=True) + EPS)
                outs[grp][:, lo + hh * GDN_DK:lo + (hh + 1) * GDN_DK] = (yh * (inv * mult)).astype(BF16)

    gla_slab = 2 * GLA_DK

    def gla_item(j):
        kc = slice(j * gla_slab, (j + 1) * gla_slab)
        q = proj(W_AQ + j * gla_slab, W_AQ + (j + 1) * gla_slab) * (GLA_DK ** -0.5)
        k = proj(W_AK + j * gla_slab, W_AK + (j + 1) * gla_slab)
        bh = b[:, kc]
        qd_ref[:, kc] = (q * jnp.exp(bh)).astype(BF16)
        ki_ref[:, kc] = (k * jnp.exp(-bh)).astype(BF16)
        for c in range(nchunk):
            r = slice(c * CHUNK, (c + 1) * CHUNK)
            last = bh[c * CHUNK + CHUNK - 1:(c + 1) * CHUNK]
            ks_ref[r, kc] = (k[r] * jnp.exp(last - bh[r])).astype(BF16)
            dec_ref[c:c + 1, kc] = jnp.exp(last)

    plain_slab = 512

    def plain_item(out_ref, w_lo, j):
        out_ref[:, j * plain_slab:(j + 1) * plain_slab] = proj(
            w_lo + j * plain_slab, w_lo + (j + 1) * plain_slab).astype(BF16)

    heavy = [lambda j=j: conv_item(j) for j in range(3 * GDN_QK // conv_slab)]
    heavy += [lambda j=j: gla_item(j) for j in range(GLA_QK // gla_slab)]
    light = [lambda o=o, w=w, j=j: plain_item(o, w, j)
             for o, w in ((va_ref, W_AV), (ga_ref, W_AG), (gd_ref, W_DG), (ma_ref, W_MA), (md_ref, W_MD))
             for j in range(D_MODEL // plain_slab)]
    for i in range(max(len(heavy), len(light))):
        if i < len(heavy):
            heavy[i]()
        if i < len(light):
            light[i]()


def _odd_rows(x, m):
    return jnp.concatenate([x[(2 * j + 1) * m:(2 * j + 2) * m] for j in range(GROUP // (2 * m))], axis=0)


def _scatter_odd(xh, m):
    zero = jnp.zeros((m, xh.shape[1]), xh.dtype)
    parts = []
    for j in range(GROUP // (2 * m)):
        parts += [zero, xh[j * m:(j + 1) * m]]
    return jnp.concatenate(parts, axis=0)


def _mix_kernel(qd_ref, ki_ref, ks_ref, va_ref, ga_ref, dec_ref,
                q_ref, k_ref, v_ref, gd_ref, sm_ref, smt_ref, na_ref, nd_ref,
                oa_ref, od_ref, sa_ref, sd_ref):
    @pl.when(pl.program_id(1) == 0)
    def _():
        sa_ref[...] = jnp.zeros_like(sa_ref)
        sd_ref[...] = jnp.zeros_like(sd_ref)

    row = lax.broadcasted_iota(jnp.int32, (GROUP, GROUP), 0)
    col = lax.broadcasted_iota(jnp.int32, (GROUP, GROUP), 1)
    log_chunk = CHUNK.bit_length() - 1
    same = (row >> log_chunk) == (col >> log_chunk)
    causal = same & (col <= row)
    strict = same & (col < row)
    eye = (row == col).astype(F32)
    level_masks = []
    for lg in range(log_chunk):
        level_masks.append(((row >> (lg + 1)) == (col >> (lg + 1)))
                           & (((row >> lg) & 1) == 1) & (((col >> lg) & 1) == 0))

    a_heads = range(GLA_HEADS)
    d_heads = range(GDN_HEADS)
    kcs = [slice(h * GLA_DK, (h + 1) * GLA_DK) for h in a_heads]
    vcs = [slice(h * GLA_DV, (h + 1) * GLA_DV) for h in a_heads]
    hcs = [slice(h * GDN_DK, (h + 1) * GDN_DK) for h in d_heads]
    sm = sm_ref[...]

    qd_a, ks_a, v_a, o_a = {}, {}, {}, {}

    def gla_prep(h):
        qd_a[h] = qd_ref[:, kcs[h]]
        ks_a[h] = ks_ref[:, kcs[h]]
        v_a[h] = va_ref[:, vcs[h]]
        att = jnp.where(causal, _dot_nt(qd_a[h], ki_ref[:, kcs[h]]), 0.0).astype(BF16)
        o_a[h] = _dot(att, v_a[h])

    a_mat, att_d, rhs, qd_d, kst_d, tinv = {}, {}, {}, {}, {}, {}

    def gdn_prep(h):
        beta = sm[:, SM_BETA + h:SM_BETA + h + 1]
        gc_c = sm[:, SM_GC + h:SM_GC + h + 1]
        egc = sm[:, SM_EGC + h:SM_EGC + h + 1]
        ekl = sm[:, SM_EKL + h:SM_EKL + h + 1]
        gc_r = smt_ref[SM_GC + h:SM_GC + h + 1, :]
        q = q_ref[:, hcs[h]]
        k = k_ref[:, hcs[h]]
        kf = k.astype(F32)
        kb = kf * beta
        decay = jnp.exp(jnp.where(causal, gc_c - gc_r, NEG_BIG))
        a_mat[h] = jnp.where(strict, _dot_nt(kb.astype(BF16), k) * decay, 0.0)
        att = _dot_nt(q, k) * decay
        att_d[h] = (att[:, :GROUP // 2] + att[:, GROUP // 2:]).astype(BF16)
        rhs[h] = jnp.concatenate([(v_ref[:, hcs[h]].astype(F32) * beta).astype(BF16),
                                  (kb * egc).astype(BF16)], axis=1)
        qd_d[h] = (q.astype(F32) * egc).astype(BF16)
        kst_d[h] = (kf * ekl).astype(BF16)
        tinv[h] = eye - jnp.where(level_masks[0], a_mat[h], 0.0)

    def inv_level(heads, lg):
        m = 1 << lg
        lvl = level_masks[lg]
        tb = {h: tinv[h].astype(BF16) for h in heads}
        if m % SUBLANES:
            x_mat = {h: _dot(jnp.where(lvl, a_mat[h], 0.0).astype(BF16), tb[h]).astype(BF16) for h in heads}
            for h in heads:
                tinv[h] = tinv[h] - _dot(tb[h], x_mat[h])
        else:
            x_half = {h: _dot(_odd_rows(jnp.where(lvl, a_mat[h], 0.0), m).astype(BF16), tb[h]) for h in heads}
            upd = {h: _dot(_odd_rows(tinv[h], m).astype(BF16), _scatter_odd(x_half[h], m).astype(BF16))
                   for h in heads}
            for h in heads:
                tinv[h] = tinv[h] - _scatter_odd(upd[h], m)

    half = GDN_HEADS // 2
    set_a, set_b = list(d_heads)[:half], list(d_heads)[half:]
    levels = list(range(1, log_chunk))
    for h in set_a:
        gdn_prep(h)
    for i, lg in enumerate(levels):
        inv_level(set_a, lg)
        if i < len(set_b):
            gdn_prep(set_b[i])
    for i, lg in enumerate(levels):
        inv_level(set_b, lg)
        if i < GLA_HEADS:
            gla_prep(i)

    sol = [_dot(tinv[h].astype(BF16), rhs[h]) for h in d_heads]
    u_d = [sol[h][:, :GDN_DV] for h in d_heads]
    w_d = [sol[h][:, GDN_DV:].astype(BF16) for h in d_heads]

    st_a = [sa_ref[h] for h in a_heads]
    st_d = [sd_ref[h] for h in d_heads]
    dec8 = jnp.concatenate([dec_ref[0], jnp.zeros((SUBLANES - NSUB, GLA_QK), F32)], axis=0)
    dec_t = [dec8[:, kcs[h]].T for h in a_heads]
    zero_half = jnp.zeros((CHUNK, GDN_DV), BF16)
    for c in range(NSUB):
        r = slice(c * CHUNK, (c + 1) * CHUNK)
        sb = [st_d[h].astype(BF16) for h in d_heads]
        ws = [_dot(jnp.concatenate([w_d[h][r], qd_d[h][r]], axis=0), sb[h]) for h in d_heads]
        for h in a_heads:
            o = o_a[h][r] + _dot(qd_a[h][r], st_a[h].astype(BF16))
            st_a[h] = st_a[h] * dec_t[h][:, c:c + 1] + _dot_tn(ks_a[h][r], v_a[h][r])
            o = _rms(o, na_ref[...]) * _silu(ga_ref[r, vcs[h]].astype(F32))
            oa_ref[r, vcs[h]] = o.astype(BF16)
        for h in d_heads:
            vn = (u_d[h][r] - ws[h][:CHUNK]).astype(BF16)
            vn_pad = jnp.concatenate([zero_half, vn] if c % 2 else [vn, zero_half], axis=0)
            o = ws[h][CHUNK:] + _dot(att_d[h][r], vn_pad)
            dec = sm[(c + 1) * CHUNK - 1:(c + 1) * CHUNK, SM_DEC + h:SM_DEC + h + 1]
            st_d[h] = st_d[h] * dec + _dot_tn(kst_d[h][r], vn)
            o = _rms(o, nd_ref[...]) * _silu(gd_ref[r, hcs[h]].astype(F32))
            od_ref[r, hcs[h]] = o.astype(BF16)
    for h in a_heads:
        sa_ref[h] = st_a[h]
    for h in d_heads:
        sd_ref[h] = st_d[h]


def _merge_kernel(oa_ref, od_ref, ma_ref, md_ref, x_ref, mod_ref, nw_ref,
                  wa_ref, wd_ref, wo_ref, o_ref):
    ya = _dot(oa_ref[...], wa_ref[...])
    yd = _dot(od_ref[...], wd_ref[...])
    merged = _sigmoid(ma_ref[...].astype(F32)) * ya + _sigmoid(md_ref[...].astype(F32)) * yd
    z = _dot(merged.astype(BF16), wo_ref[...])
    gate = mod_ref[0][2:3]
    o_ref[...] = x_ref[...] + gate * _rms(z, nw_ref[...])


def _mlp_kernel(x_ref, mod_ref, nw1_ref, nw2_ref, w1_ref, w2_ref, o_ref):
    x = x_ref[...]
    mod = mod_ref[0]
    h = _modulated_norm(x, mod, nw1_ref[...], 3).astype(BF16)
    ff_tile = 1024
    y = jnp.zeros(x.shape, F32)
    for j in range(D_FF // ff_tile):
        fc = slice(j * ff_tile, (j + 1) * ff_tile)
        a = jnp.maximum(_dot(h, w1_ref[:, fc]), 0.0)
        y = y + _dot((a * a).astype(BF16), w2_ref[fc, :])
    o_ref[...] = x + mod[5:6] * _rms(y, nw2_ref[...])


def _const_spec(shape):
    nd = len(shape)
    return pl.BlockSpec(shape, lambda *_: (0,) * nd, pipeline_mode=pl.Buffered(1))


def _params(sem):
    return pltpu.CompilerParams(dimension_semantics=sem, vmem_limit_bytes=VMEM_LIMIT)


def kernel(x, c, ada_w, ada_b, pre_mix_w, post_mix_w, pre_mlp_w, post_mlp_w, w_in, gla_w_lr, gla_b_lr, gla_onorm_w, gdn_conv_w, gdn_a_log, gdn_dt_bias, gdn_onorm_w, w_branch_gla, w_branch_gdn, w_out, mlp_w1, mlp_w2):
    bsz, seq, d = x.shape
    depth = ada_w.shape[0]
    assert d == D_MODEL and seq % TOKEN_TILE == 0 and seq % GROUP == 0
    tm = TOKEN_TILE
    nt = seq // tm
    tokens = bsz * seq
    ng = seq // GROUP

    ri = lax.broadcasted_iota(jnp.int32, (tm, tm), 0)
    ci = lax.broadcasted_iota(jnp.int32, (tm, tm), 1)
    tri = ((ri // CHUNK == ci // CHUNK) & (ci <= ri)).astype(BF16)

    xf = x.reshape(tokens, d)
    c_pad = jnp.pad(c, ((0, SUBLANES - bsz), (0, 0)))
    row = lambda v: v.reshape(1, -1)

    tile_map = lambda b, i: (b * nt + i, 0)
    grp_map = lambda b, i: (b * ng + i, 0)
    mod_spec = pl.BlockSpec((1, 6, d), lambda b, i: (b, 0, 0))

    def tile_spec(width):
        return pl.BlockSpec((tm, width), tile_map)

    def grp_spec(width):
        return pl.BlockSpec((GROUP, width), grp_map)

    def bf16_out(width):
        return jax.ShapeDtypeStruct((tokens, width), BF16)

    for l in range(depth):
        mod = _ada(c_pad, ada_w[l], row(ada_b[l]))[:bsz].reshape(bsz, 6, d)

        wi = w_in[l]
        o = [0]
        def take(n):
            s = wi[:, o[0]:o[0] + n]
            o[0] += n
            return s
        a_q, a_k, a_v, a_g, a_lr = take(GLA_QK), take(GLA_QK), take(GLA_V), take(GLA_V), take(GLA_RANK)
        d_qkv, d_g, d_beta, d_a = take(3 * GDN_QK), take(GDN_V), take(GDN_HEADS), take(GDN_HEADS)
        m_a, m_d = take(d), take(d)
        lr_pad = jnp.pad(a_lr, ((0, 0), (0, LANES - GLA_RANK)))
        sm_cols = jnp.concatenate([d_beta] + [d_a] * 4, axis=1)
        sm_pad = jnp.pad(sm_cols, ((0, 0), (0, LANES - sm_cols.shape[1])))
        w_all = jnp.concatenate([a_q, a_k, a_v, a_g, m_a, m_d, lr_pad, sm_pad, d_qkv, d_g], axis=1).astype(BF16)
        wlr = jnp.pad(gla_w_lr[l], ((0, LANES - GLA_RANK), (0, 0))).astype(BF16)
        lane_pad = (SM_GC, LANES - SM_GC - 4 * GDN_HEADS)
        avec = row(jnp.pad(jnp.tile(jnp.exp(gdn_a_log[l]), 4), lane_pad))
        dtvec = row(jnp.pad(jnp.tile(gdn_dt_bias[l], 4), lane_pad))

        qd, ki, ks, va, ga, ma, md, dec, qg, kg, vg, gd, sm, smt = pl.pallas_call(
            _proj_in_kernel,
            grid=(bsz, nt),
            in_specs=[tile_spec(d), mod_spec, _const_spec((1, d)), _const_spec((d, W_END)),
                      _const_spec((LANES, GLA_QK)), _const_spec((1, GLA_QK)),
                      _const_spec((CONV_K, 3 * GDN_QK)), _const_spec((1, LANES)), _const_spec((1, LANES)),
                      _const_spec((tm, tm))],
            out_specs=[tile_spec(GLA_QK)] * 3 + [tile_spec(GLA_V)] * 2 + [tile_spec(d)] * 2
                      + [pl.BlockSpec((tm // CHUNK, GLA_QK), tile_map)]
                      + [tile_spec(GDN_QK)] * 4
                      + [tile_spec(LANES), pl.BlockSpec((LANES, tm), lambda b, i: (0, b * nt + i))],
            out_shape=[bf16_out(GLA_QK)] * 3 + [bf16_out(GLA_V)] * 2 + [bf16_out(d)] * 2
                      + [jax.ShapeDtypeStruct((tokens // CHUNK, GLA_QK), F32)]
                      + [bf16_out(GDN_QK)] * 4
                      + [jax.ShapeDtypeStruct((tokens, LANES), F32),
                         jax.ShapeDtypeStruct((LANES, tokens), F32)],
            scratch_shapes=[pltpu.VMEM((SUBLANES, 3 * GDN_QK), F32)],
            compiler_params=_params(("arbitrary", "arbitrary")),
            name="proj_in",
        )(xf, mod, row(pre_mix_w[l]), w_all, wlr, row(gla_b_lr[l]), gdn_conv_w[l], avec, dtvec, tri)

        oa, od = pl.pallas_call(
            _mix_kernel,
            grid=(bsz, ng),
            in_specs=[grp_spec(GLA_QK)] * 3 + [grp_spec(GLA_V)] * 2
                     + [pl.BlockSpec((1, NSUB, GLA_QK), lambda b, i: (b * ng + i, 0, 0))]
                     + [grp_spec(GDN_QK)] * 4 + [grp_spec(LANES),
                        pl.BlockSpec((LANES, GROUP), lambda b, i: (0, b * ng + i)),
                        _const_spec((1, GLA_DV)), _const_spec((1, GDN_DV))],
            out_specs=[grp_spec(GLA_V), grp_spec(GDN_V)],
            out_shape=[bf16_out(GLA_V), bf16_out(GDN_V)],
            scratch_shapes=[pltpu.VMEM((GLA_HEADS, GLA_DK, GLA_DV), F32),
                            pltpu.VMEM((GDN_HEADS, GDN_DK, GDN_DV), F32)],
            compiler_params=_params(("arbitrary", "arbitrary")),
            name="mix",
        )(qd, ki, ks, va, ga, dec.reshape(tokens // GROUP, NSUB, GLA_QK),
          qg, kg, vg, gd, sm, smt, row(gla_onorm_w[l]), row(gdn_onorm_w[l]))

        x1 = pl.pallas_call(
            _merge_kernel,
            grid=(bsz, nt),
            in_specs=[tile_spec(d)] * 5 + [mod_spec, _const_spec((1, d))] + [_const_spec((d, d))] * 3,
            out_specs=tile_spec(d),
            out_shape=jax.ShapeDtypeStruct((tokens, d), F32),
            compiler_params=_params(("parallel", "parallel")),
            name="merge",
        )(oa, od, ma, md, xf, mod, row(post_mix_w[l]),
          w_branch_gla[l].astype(BF16), w_branch_gdn[l].astype(BF16), w_out[l].astype(BF16))

        xf = pl.pallas_call(
            _mlp_kernel,
            grid=(bsz, nt),
            in_specs=[tile_spec(d), mod_spec, _const_spec((1, d)), _const_spec((1, d)),
                      _const_spec((d, D_FF)), _const_spec((D_FF, d))],
            out_specs=tile_spec(d),
            out_shape=jax.ShapeDtypeStruct((tokens, d), F32),
            compiler_params=_params(("parallel", "parallel")),
            name="mlp",
        )(x1, mod, row(pre_mlp_w[l]), row(post_mlp_w[l]), mlp_w1[l].astype(BF16), mlp_w2[l].astype(BF16))

    return xf.reshape(bsz, seq, d)
```
